```python
import math
import jax, jax.numpy as jnp
from jax import lax
import numpy as np

D_MODEL = 1024
BATCH = 16
SEQ = 2048
DEPTH = 1

MEM_LEN = 256
EPS = 1e-6
NEG_INF = -1e30
CONV_WIDTH = 512
CONV_KERNEL = 31
NSA_HEADS = 8
HEAD_DIM = 64
NSA_WIDTH = NSA_HEADS * HEAD_DIM
NSA_KV_GROUPS = 2
HEADS_PER_GROUP = NSA_HEADS // NSA_KV_GROUPS
KV_WIDTH = NSA_KV_GROUPS * HEAD_DIM
MIX_WIDTH = CONV_WIDTH + NSA_WIDTH
ROT_DIM = HEAD_DIM // 4
ROPE_THETA = 500000.0
CMP_BLOCK = 32
CMP_STRIDE = 16
CMP_HIDDEN = 128
SEL_BLOCK = 64
N_SELECT = 16
N_LOCAL_SEL = 2
FORCED_SCORE = 1e9
WINDOW = 512
WIN_Q_BLOCK = 128
SEL_Q_CHUNK = 32
MEM_HEADS = 4
MEM_HEAD_DIM = D_MODEL // MEM_HEADS
PEER_HEADS = 8
PEER_KEYS = 128
PEER_EXPERTS = PEER_KEYS * PEER_KEYS
PEER_QDIM = 256
PEER_HALF = PEER_QDIM // 2
PEER_TOPK = 16
PEER_TOKEN_CHUNK = 128
IN_SIZES = [CONV_WIDTH, CONV_WIDTH, NSA_WIDTH,
            KV_WIDTH, KV_WIDTH, KV_WIDTH, KV_WIDTH, KV_WIDTH, KV_WIDTH,
            3 * NSA_HEADS]
IN_COLS = sum(IN_SIZES)
IN_SPLITS = [sum(IN_SIZES[:i + 1]) for i in range(len(IN_SIZES) - 1)]

kernel_name = "hybrid_conv_nsa_peer_block"


def rmsnorm(x, g):
    xf = x.astype(jnp.float32)
    y = xf * lax.rsqrt(jnp.mean(xf * xf, axis=-1, keepdims=True) + EPS)
    return (y * g.astype(jnp.float32)).astype(x.dtype)


def layernorm(x, g, b):
    xf = x.astype(jnp.float32)
    mu = jnp.mean(xf, axis=-1, keepdims=True)
    var = jnp.mean(jnp.square(xf - mu), axis=-1, keepdims=True)
    y = (xf - mu) * lax.rsqrt(var + EPS) * g.astype(jnp.float32) + b.astype(jnp.float32)
    return y.astype(x.dtype)


def masked_softmax(s, mask):
    s = jnp.where(mask, s.astype(jnp.float32), NEG_INF)
    return jax.nn.softmax(s, axis=-1) * mask


def partial_rope(x, positions):
    half = ROT_DIM // 2
    inv = ROPE_THETA ** (-(jnp.arange(half, dtype=jnp.float32) * 2.0 / ROT_DIM))
    ang = positions.astype(jnp.float32)[..., None] * inv
    cos = jnp.cos(ang)[:, :, None, :]
    sin = jnp.sin(ang)[:, :, None, :]
    xr = x[..., :ROT_DIM].astype(jnp.float32)
    x1, x2 = xr[..., :half], xr[..., half:]
    rot = jnp.concatenate([x1 * cos - x2 * sin, x2 * cos + x1 * sin], axis=-1).astype(x.dtype)
    return jnp.concatenate([rot, x[..., ROT_DIM:]], axis=-1)


def conformer_conv(a, b, w_dw, b_dw, ln_g, ln_b):
    u = a * jax.nn.sigmoid(b)
    y = lax.conv_general_dilated(
        u, w_dw[:, None, :], window_strides=(1,), padding=[(CONV_KERNEL - 1, 0)],
        dimension_numbers=('NWC', 'WIO', 'NWC'), feature_group_count=CONV_WIDTH)
    y = layernorm(y + b_dw, ln_g, ln_b)
    return jax.nn.silu(y)


def compress_tokens(kv, pos_emb, w1, b1, w2, b2):
    B, T, G, dh = kv.shape
    n_cmp = (T - CMP_BLOCK) // CMP_STRIDE + 1
    idx = jnp.arange(n_cmp)[:, None] * CMP_STRIDE + jnp.arange(CMP_BLOCK)[None, :]
    blk = kv[:, idx] + pos_emb[:, None, :]
    blk = blk.transpose(0, 1, 3, 2, 4).reshape(B, n_cmp, G, CMP_BLOCK * dh)
    hdn = jax.nn.gelu(blk @ w1 + b1)
    return hdn @ w2 + b2


def nsa_mixer(q, k_c, v_c, k_s, v_s, k_w, v_w, gate_logits, positions,
              cmp_pos, cmp_w1, cmp_b1, cmp_w2, cmp_b2):
    B, T = q.shape[:2]
    G, Hg, dh = NSA_KV_GROUPS, HEADS_PER_GROUP, HEAD_DIM
    scale = dh ** -0.5
    t_idx = jnp.arange(T)
    kvr = lambda t: t.reshape(B, T, G, dh)
    k_c, v_c, k_s, v_s, k_w, v_w = map(kvr, (k_c, v_c, k_s, v_s, k_w, v_w))

    qg = q.reshape(B, T, G, Hg, dh)
    kc = compress_tokens(k_c, cmp_pos[0], cmp_w1[0], cmp_b1[0], cmp_w2[0], cmp_b2[0])
    vc = compress_tokens(v_c, cmp_pos[1], cmp_w1[1], cmp_b1[1], cmp_w2[1], cmp_b2[1])
    n_cmp = kc.shape[1]
    c_start = jnp.arange(n_cmp) * CMP_STRIDE
    s_c = jnp.einsum('btghd,bcgd->bghtc', qg, kc) * scale
    p_c = masked_softmax(s_c, (c_start + CMP_BLOCK - 1)[None, :] <= t_idx[:, None])
    o_cmp = jnp.einsum('bghtc,bcgd->btghd', p_c.astype(vc.dtype), vc)

    n_blk = T // SEL_BLOCK
    s_start = jnp.arange(n_blk) * SEL_BLOCK
    overlap = ((c_start[:, None] <= s_start[None, :] + SEL_BLOCK - 1)
               & (c_start[:, None] + CMP_BLOCK - 1 >= s_start[None, :])).astype(jnp.float32)
    imp = jnp.einsum('bghtc,cs->bgts', p_c, overlap)
    cur = t_idx // SEL_BLOCK
    blk = jnp.arange(n_blk)
    valid = blk[None, :] <= cur[:, None]
    dist = cur[:, None] - blk[None, :]
    forced = (blk[None, :] == 0) | ((dist >= 0) & (dist < N_LOCAL_SEL))
    imp = jnp.where(forced, FORCED_SCORE, jnp.where(valid, imp, -1.0))
    n_sel = min(N_SELECT, n_blk)
    _, sel_idx = lax.top_k(imp, n_sel)

    q_r = partial_rope(q.reshape(B, T, NSA_HEADS, dh), positions).reshape(B, T, G, Hg, dh)
    k_s = partial_rope(k_s, positions)
    k_w = partial_rope(k_w, positions)

    k_blocks = k_s.reshape(B, n_blk, SEL_BLOCK, G, dh).transpose(0, 3, 1, 2, 4)
    v_blocks = v_s.reshape(B, n_blk, SEL_BLOCK, G, dh).transpose(0, 3, 1, 2, 4)
    b_ix = jnp.arange(B)[:, None, None, None]
    g_ix = jnp.arange(G)[None, :, None, None]
    n_keys_sel = n_sel * SEL_BLOCK

    def sel_chunk(ci):
        s0 = ci * SEL_Q_CHUNK
        qc = lax.dynamic_slice_in_dim(q_r, s0, SEL_Q_CHUNK, axis=1)
        ic = lax.dynamic_slice_in_dim(sel_idx, s0, SEL_Q_CHUNK, axis=2)
        kg = k_blocks[b_ix, g_ix, ic].reshape(B, G, SEL_Q_CHUNK, n_keys_sel, dh)
        vg = v_blocks[b_ix, g_ix, ic].reshape(B, G, SEL_Q_CHUNK, n_keys_sel, dh)
        kpos = (ic[..., None] * SEL_BLOCK + jnp.arange(SEL_BLOCK)).reshape(B, G, SEL_Q_CHUNK, n_keys_sel)
        qpos = s0 + jnp.arange(SEL_Q_CHUNK)
        s = jnp.einsum('bqghd,bgqkd->bghqk', qc, kg) * scale
        p = masked_softmax(s, (kpos <= qpos[None, None, :, None])[:, :, None])
        return jnp.einsum('bghqk,bgqkd->bqghd', p.astype(vg.dtype), vg)

    o_slc = lax.map(sel_chunk, jnp.arange(T // SEL_Q_CHUNK))
    o_slc = jnp.moveaxis(o_slc, 0, 1).reshape(B, T, G, Hg, dh)

    k_pad = jnp.pad(k_w, ((0, 0), (WINDOW, 0), (0, 0), (0, 0)))
    v_pad = jnp.pad(v_w, ((0, 0), (WINDOW, 0), (0, 0), (0, 0)))
    band = WINDOW + WIN_Q_BLOCK

    def win_block(bi):
        s0 = bi * WIN_Q_BLOCK
        qb = lax.dynamic_slice_in_dim(q_r, s0, WIN_Q_BLOCK, axis=1)
        kb = lax.dynamic_slice_in_dim(k_pad, s0, band, axis=1)
        vb = lax.dynamic_slice_in_dim(v_pad, s0, band, axis=1)
        kpos = s0 - WINDOW + jnp.arange(band)
        qpos = s0 + jnp.arange(WIN_Q_BLOCK)
        mask = ((kpos[None, :] <= qpos[:, None]) & (kpos[None, :] > qpos[:, None] - WINDOW)
                & (kpos[None, :] >= 0))
        s = jnp.einsum('bqghd,bkgd->bghqk', qb, kb) * scale
        p = masked_softmax(s, mask)
        return jnp.einsum('bghqk,bkgd->bqghd', p.astype(vb.dtype), vb)

    o_win = lax.map(win_block, jnp.arange(T // WIN_Q_BLOCK))
    o_win = jnp.moveaxis(o_win, 0, 1).reshape(B, T, G, Hg, dh)

    g = jax.nn.sigmoid(gate_logits).reshape(B, T, G, Hg, 3)
    out = g[..., 0:1] * o_cmp + g[..., 1:2] * o_slc + g[..., 2:3] * o_win
    return out.reshape(B, T, NSA_WIDTH)


def memory_attention(hn, mem_n, w_q, w_k, w_v, w_o):
    B, T, D = hn.shape
    M = mem_n.shape[1]
    q = (hn @ w_q).reshape(B, T, MEM_HEADS, MEM_HEAD_DIM)
    k = (mem_n @ w_k).reshape(B, M, MEM_HEADS, MEM_HEAD_DIM)
    v = (mem_n @ w_v).reshape(B, M, MEM_HEADS, MEM_HEAD_DIM)
    s = jnp.einsum('bthd,bmhd->bhtm', q, k).astype(jnp.float32) * (MEM_HEAD_DIM ** -0.5)
    p = jax.nn.softmax(s, axis=-1).astype(v.dtype)
    o = jnp.einsum('bhtm,bmhd->bthd', p, v).reshape(B, T, D)
    return o @ w_o


def peer(hn, w_q, sub_keys, u_emb, v_emb):
    B, T, D = hn.shape
    q = (hn @ w_q).reshape(B, T, PEER_HEADS, 2, PEER_HALF)
    s = jnp.einsum('bthpd,hpkd->bthpk', q, sub_keys).astype(jnp.float32)
    s_top, i_top = lax.top_k(s, PEER_TOPK)
    cand = s_top[..., 0, :, None] + s_top[..., 1, None, :]
    cand_idx = i_top[..., 0, :, None] * PEER_KEYS + i_top[..., 1, None, :]
    n_cand = PEER_TOPK * PEER_TOPK
    best, pos = lax.top_k(cand.reshape(B, T, PEER_HEADS, n_cand), PEER_TOPK)
    experts = jnp.take_along_axis(cand_idx.reshape(B, T, PEER_HEADS, n_cand), pos, axis=-1)
    gates = jax.nn.softmax(best, axis=-1).astype(hn.dtype)
    n_tok = B * T
    n_ret = PEER_HEADS * PEER_TOPK
    xs = hn.reshape(n_tok // PEER_TOKEN_CHUNK, PEER_TOKEN_CHUNK, D)
    es = experts.reshape(n_tok // PEER_TOKEN_CHUNK, PEER_TOKEN_CHUNK, n_ret)
    gs = gates.reshape(n_tok // PEER_TOKEN_CHUNK, PEER_TOKEN_CHUNK, n_ret)

    def chunk(args):
        xc, ec, gc = args
        u = u_emb[ec]
        a = jax.nn.gelu(jnp.einsum('nd,nkd->nk', xc, u))
        return jnp.einsum('nk,nkd->nd', gc * a, v_emb[ec])

    y = lax.map(chunk, (xs, es, gs))
    return y.reshape(B, T, D)


def setup_inputs(seed: int = 0) -> dict:
    key = jax.random.key(seed)
    ks = jax.random.split(key, 32)
    L, D = DEPTH, D_MODEL
    nrm = lambda k, shape, sc: jax.random.normal(k, shape, jnp.float32) * sc
    gain = lambda k, shape: 1.0 + 0.02 * jax.random.normal(k, shape, jnp.float32)
    return {
        "x": nrm(ks[0], (BATCH, SEQ, D), 1.0),
        "mem": nrm(ks[1], (BATCH, MEM_LEN, D), 1.0),
        "positions": jnp.broadcast_to(jnp.arange(SEQ, dtype=jnp.int32), (BATCH, SEQ)),
        "mix_norm_g": gain(ks[2], (L, D)),
        "w_in": nrm(ks[3], (L, D, IN_COLS), D ** -0.5),
        "conv_dw_w": nrm(ks[4], (L, CONV_KERNEL, CONV_WIDTH), CONV_KERNEL ** -0.5),
        "conv_dw_b": nrm(ks[5], (L, CONV_WIDTH), 0.02),
        "conv_ln_g": gain(ks[6], (L, CONV_WIDTH)),
        "conv_ln_b": nrm(ks[7], (L, CONV_WIDTH), 0.02),
        "cmp_pos": nrm(ks[8], (L, 2, CMP_BLOCK, HEAD_DIM), 0.02),
        "cmp_w1": nrm(ks[9], (L, 2, CMP_BLOCK * HEAD_DIM, CMP_HIDDEN), (CMP_BLOCK * HEAD_DIM) ** -0.5),
        "cmp_b1": nrm(ks[10], (L, 2, CMP_HIDDEN), 0.02),
        "cmp_w2": nrm(ks[11], (L, 2, CMP_HIDDEN, HEAD_DIM), CMP_HIDDEN ** -0.5),
        "cmp_b2": nrm(ks[12], (L, 2, HEAD_DIM), 0.02),
        "w_out": nrm(ks[13], (L, MIX_WIDTH, D), MIX_WIDTH ** -0.5),
        "mem_q_norm_g": gain(ks[14], (L, D)),
        "mem_kv_norm_g": gain(ks[15], (L, D)),
        "w_mem_q": nrm(ks[16], (L, D, D), D ** -0.5),
        "w_mem_k": nrm(ks[17], (L, D, D), D ** -0.5),
        "w_mem_v": nrm(ks[18], (L, D, D), D ** -0.5),
        "w_mem_o": nrm(ks[19], (L, D, D), D ** -0.5),
        "peer_norm_g": gain(ks[20], (L, D)),
        "peer_w_q": nrm(ks[21], (L, D, PEER_HEADS * PEER_QDIM), D ** -0.5),
        "peer_sub_keys": nrm(ks[22], (L, PEER_HEADS, 2, PEER_KEYS, PEER_HALF), PEER_HALF ** -0.5),
        "peer_u": nrm(ks[23], (L, PEER_EXPERTS, D), D ** -0.5),
        "peer_v": nrm(ks[24], (L, PEER_EXPERTS, D), D ** -0.5),
        "final_norm_g": gain(ks[25], (D,)),
    }


def reference(x, mem, positions, mix_norm_g, w_in, conv_dw_w, conv_dw_b, conv_ln_g, conv_ln_b,
              cmp_pos, cmp_w1, cmp_b1, cmp_w2, cmp_b2, w_out,
              mem_q_norm_g, mem_kv_norm_g, w_mem_q, w_mem_k, w_mem_v, w_mem_o,
              peer_norm_g, peer_w_q, peer_sub_keys, peer_u, peer_v, final_norm_g):
    h = x
    for l in range(DEPTH):
        hn = rmsnorm(h, mix_norm_g[l])
        proj = hn @ w_in[l]
        conv_a, conv_b, q, k_c, v_c, k_s, v_s, k_w, v_w, gate_logits = jnp.split(proj, IN_SPLITS, axis=-1)
        y_conv = conformer_conv(conv_a, conv_b, conv_dw_w[l], conv_dw_b[l], conv_ln_g[l], conv_ln_b[l])
        y_nsa = nsa_mixer(q, k_c, v_c, k_s, v_s, k_w, v_w, gate_logits, positions,
                          cmp_pos[l], cmp_w1[l], cmp_b1[l], cmp_w2[l], cmp_b2[l])
        h = h + jnp.concatenate([y_conv, y_nsa], axis=-1) @ w_out[l]
        h = h + memory_attention(rmsnorm(h, mem_q_norm_g[l]), rmsnorm(mem, mem_kv_norm_g[l]),
                                 w_mem_q[l], w_mem_k[l], w_mem_v[l], w_mem_o[l])
        h = h + peer(rmsnorm(h, peer_norm_g[l]), peer_w_q[l], peer_sub_keys[l], peer_u[l], peer_v[l])
    return rmsnorm(h, final_norm_g)
```

```python
import functools

import jax
import jax.numpy as jnp
from jax import lax
from jax.experimental import pallas as pl
from jax.experimental.pallas import tpu as pltpu

F32 = jnp.float32
BF16 = jnp.bfloat16

D_MODEL = 1024
EPS = 1e-6
CONV_WIDTH = 512
CONV_KERNEL = 31
NSA_HEADS = 8
HEAD_DIM = 64
KV_GROUPS = 2
HEADS_PER_GROUP = NSA_HEADS // KV_GROUPS
KV_WIDTH = KV_GROUPS * HEAD_DIM
ROT_DIM = HEAD_DIM // 4
ROT_HALF = ROT_DIM // 2
ROPE_THETA = 500000.0
LANES = 128
VMEM_LIMIT = 56 * 1024 * 1024


def _rms(x, g):
    return (x * lax.rsqrt(jnp.mean(x * x, axis=-1, keepdims=True) + EPS)) * g


def _rope_blocks(v, c, s1, s2):
    outs = []
    for j in range(v.shape[1] // LANES):
        blk = v[:, j * LANES:(j + 1) * LANES]
        outs.append(blk * c + pltpu.roll(blk, LANES - ROT_HALF, 1) * s1 + pltpu.roll(blk, ROT_HALF, 1) * s2)
    return outs


def _inproj_body(x_ref, posb_ref, g_ref, inv_ref, wab_ref, wq_ref, wkv_ref, wg_ref,
                 u_ref, q_ref, qr_ref, kc_ref, vc_ref, ks_ref, vs_ref, kw_ref, vw_ref, gate_ref):
    hb = _rms(x_ref[...], g_ref[...]).astype(BF16)
    ab = jnp.dot(hb, wab_ref[...], preferred_element_type=F32)
    u_ref[...] = ab[:, :CONV_WIDTH] * jax.nn.sigmoid(ab[:, CONV_WIDTH:])

    ang = posb_ref[...] * inv_ref[...]
    c = jnp.cos(ang)
    s = jnp.sin(ang)
    lane = lax.broadcasted_iota(jnp.int32, (1, LANES), 1) % HEAD_DIM
    s1 = jnp.where(lane < ROT_HALF, -s, 0.0)
    s2 = jnp.where((lane >= ROT_HALF) & (lane < ROT_DIM), s, 0.0)

    q = jnp.dot(hb, wq_ref[...], preferred_element_type=F32) * (HEAD_DIM ** -0.5)
    q_ref[...] = q.astype(BF16)
    for j, blk in enumerate(_rope_blocks(q, c, s1, s2)):
        qr_ref[:, j * LANES:(j + 1) * LANES] = blk.astype(BF16)

    kv = jnp.dot(hb, wkv_ref[...], preferred_element_type=F32)
    kc_ref[...] = kv[:, 0 * KV_WIDTH:1 * KV_WIDTH]
    vc_ref[...] = kv[:, 1 * KV_WIDTH:2 * KV_WIDTH]
    ks_ref[...] = _rope_blocks(kv[:, 2 * KV_WIDTH:3 * KV_WIDTH], c, s1, s2)[0].astype(BF16)
    vs_ref[...] = kv[:, 3 * KV_WIDTH:4 * KV_WIDTH].astype(BF16)
    kw_ref[...] = _rope_blocks(kv[:, 4 * KV_WIDTH:5 * KV_WIDTH], c, s1, s2)[0].astype(BF16)
    vw_ref[...] = kv[:, 5 * KV_WIDTH:6 * KV_WIDTH].astype(BF16)

    gate_ref[...] = jax.nn.sigmoid(jnp.dot(hb, wg_ref[...], preferred_element_type=F32))


def _inproj(x2, posb, g, inv_lane, wab, wq_pad, wkv, wg_pad, tm=256):
    n = x2.shape[0]
    row = lambda w: pl.BlockSpec((tm, w), lambda i: (i, 0))
    full = lambda a: pl.BlockSpec(a.shape, lambda i: (0,) * a.ndim)
    qw = NSA_HEADS * LANES
    out_shape = [
        jax.ShapeDtypeStruct((n, CONV_WIDTH), F32),
        jax.ShapeDtypeStruct((n, qw), BF16),
        jax.ShapeDtypeStruct((n, qw), BF16),
        jax.ShapeDtypeStruct((n, KV_WIDTH), F32),
        jax.ShapeDtypeStruct((n, KV_WIDTH), F32),
        jax.ShapeDtypeStruct((n, KV_WIDTH), BF16),
        jax.ShapeDtypeStruct((n, KV_WIDTH), BF16),
        jax.ShapeDtypeStruct((n, KV_WIDTH), BF16),
        jax.ShapeDtypeStruct((n, KV_WIDTH), BF16),
        jax.ShapeDtypeStruct((n, LANES), F32),
    ]
    return pl.pallas_call(
        _inproj_body,
        grid=(n // tm,),
        in_specs=[row(D_MODEL), row(LANES), full(g), full(inv_lane), full(wab), full(wq_pad), full(wkv), full(wg_pad)],
        out_specs=[row(s.shape[1]) for s in out_shape],
        out_shape=out_shape,
        compiler_params=pltpu.CompilerParams(dimension_semantics=("arbitrary",), vmem_limit_bytes=VMEM_LIMIT),
        name="inproj",
    )(x2, posb, g, inv_lane, wab, wq_pad, wkv, wg_pad)


def _prep_inproj(w_in_l, positions):
    c2 = 2 * CONV_WIDTH
    wab = w_in_l[:, :c2].astype(BF16)
    wq = w_in_l[:, c2:c2 + NSA_HEADS * HEAD_DIM].reshape(D_MODEL, NSA_HEADS, HEAD_DIM)
    z = jnp.zeros_like(wq[:, :HEADS_PER_GROUP])
    wq_pad = jnp.concatenate([
        jnp.concatenate([wq[:, :HEADS_PER_GROUP], z], axis=-1),
        jnp.concatenate([z, wq[:, HEADS_PER_GROUP:]], axis=-1)], axis=1)
    wq_pad = wq_pad.reshape(D_MODEL, NSA_HEADS * LANES).astype(BF16)
    k0 = c2 + NSA_HEADS * HEAD_DIM
    wkv = w_in_l[:, k0:k0 + 6 * KV_WIDTH].astype(BF16)
    wg = w_in_l[:, k0 + 6 * KV_WIDTH:]
    wg_pad = jnp.pad(wg, ((0, 0), (0, LANES - wg.shape[1]))).astype(BF16)
    inv = ROPE_THETA ** (-(jnp.arange(ROT_HALF, dtype=F32) * 2.0 / ROT_DIM))
    lane = jnp.arange(LANES)
    inv_lane = jnp.where(lane % HEAD_DIM < ROT_DIM, inv[lane % ROT_HALF], 0.0).reshape(1, LANES).astype(F32)
    n = positions.size
    posb = jnp.broadcast_to(positions.reshape(n, 1).astype(F32), (n, LANES))
    return wab, wq_pad, wkv, wg_pad, inv_lane, posb


CONV_TOP = 32
CONV_TAIL = 8
SUBLANES = 8


def _conv_body(u_ref, w_ref, b_ref, lg_ref, lb_ref, o_ref, pad_ref, *, tt, seq):
    ti = pl.program_id(1)

    @pl.when(ti == 0)
    def _():
        pad_ref[0:CONV_TOP, :] = jnp.zeros((CONV_TOP, CONV_WIDTH), F32)
        pad_ref[CONV_TOP:CONV_TOP + seq, :] = u_ref[0]
        pad_ref[CONV_TOP + seq:, :] = jnp.zeros((CONV_TAIL, CONV_WIDTH), F32)

    r0 = pl.multiple_of(ti * tt, tt)
    win = pad_ref[pl.ds(r0, tt + CONV_TOP + CONV_TAIL), :]
    first = CONV_TOP - (CONV_KERNEL - 1)
    acc = jnp.zeros((tt, CONV_WIDTH), F32)
    for r in range(SUBLANES):
        taps = [k for k in range(CONV_KERNEL) if (first + k) % SUBLANES == r]
        if not taps:
            continue
        sh = win[r:r + tt + CONV_TOP, :]
        for k in taps:
            a = (first + k) // SUBLANES * SUBLANES
            acc = acc + sh[a:a + tt, :] * w_ref[k:k + 1, :]
    y = acc + b_ref[...]
    mu = jnp.mean(y, axis=-1, keepdims=True)
    yc = y - mu
    var = jnp.mean(yc * yc, axis=-1, keepdims=True)
    z = yc * lax.rsqrt(var + EPS) * lg_ref[...] + lb_ref[...]
    o_ref[0] = z * jax.nn.sigmoid(z)


def _conv(u3, w, b, lg, lb, tt=64):
    bsz, seq, _ = u3.shape
    full = lambda a: pl.BlockSpec(a.shape, lambda bi, ti: (0,) * a.ndim)
    return pl.pallas_call(
        functools.partial(_conv_body, tt=tt, seq=seq),
        grid=(bsz, seq // tt),
        in_specs=[pl.BlockSpec((1, seq, CONV_WIDTH), lambda bi, ti: (bi, 0, 0)), full(w), full(b), full(lg), full(lb)],
        out_specs=pl.BlockSpec((1, tt, CONV_WIDTH), lambda bi, ti: (bi, ti, 0)),
        out_shape=jax.ShapeDtypeStruct((bsz, seq, CONV_WIDTH), F32),
        scratch_shapes=[pltpu.VMEM((CONV_TOP + seq + CONV_TAIL, CONV_WIDTH), F32)],
        compiler_params=pltpu.CompilerParams(dimension_semantics=("arbitrary", "arbitrary"), vmem_limit_bytes=VMEM_LIMIT),
        name="conv",
    )(u3, w, b, lg, lb)


CMP_BLOCK = 32
CMP_STRIDE = 16
CMP_HIDDEN = 128


def _compress_body(kc_ref, vc_ref, pos_ref, w1_ref, b1_ref, w2_ref, b2_ref, ko_ref, vo_ref):
    for i, (src, dst) in enumerate(((kc_ref, ko_ref), (vc_ref, vo_ref))):
        rows = src[0]
        nrow = rows.shape[0]
        top = jnp.dot((rows + pos_ref[i, 0:1, :]).astype(BF16), w1_ref[i, 0], preferred_element_type=F32)
        bot = jnp.dot((rows + pos_ref[i, 1:2, :]).astype(BF16), w1_ref[i, 1], preferred_element_type=F32)
        h = top + pltpu.roll(bot, nrow - 1, 0) + b1_ref[i]
        out = jnp.dot(jax.nn.gelu(h).astype(BF16), w2_ref[i], preferred_element_type=F32) + b2_ref[i]
        valid = lax.broadcasted_iota(jnp.int32, out.shape, 0) < nrow - 1
        dst[0] = jnp.where(valid, out, 0.0).astype(BF16)


def _compress(kc3, vc3, pos2, w1s, b1s, w2s, b2s):
    bsz, nrow, width = kc3.shape
    full = lambda a: pl.BlockSpec(a.shape, lambda bi: (0,) * a.ndim)
    row = pl.BlockSpec((1, nrow, width), lambda bi: (bi, 0, 0))
    out = pl.BlockSpec((1, nrow, KV_WIDTH), lambda bi: (bi, 0, 0))
    return pl.pallas_call(
        _compress_body,
        grid=(bsz,),
        in_specs=[row, row, full(pos2), full(w1s), full(b1s), full(w2s), full(b2s)],
        out_specs=[out, out],
        out_shape=[jax.ShapeDtypeStruct((bsz, nrow, KV_WIDTH), BF16)] * 2,
        compiler_params=pltpu.CompilerParams(dimension_semantics=("arbitrary",), vmem_limit_bytes=VMEM_LIMIT),
        name="compress",
    )(kc3, vc3, pos2, w1s, b1s, w2s, b2s)


def _prep_compress(cmp_pos, cmp_w1, cmp_b1, cmp_w2, cmp_b2):
    half = CMP_BLOCK // 2
    eye = jnp.eye(KV_GROUPS, dtype=F32)
    w1r = cmp_w1.reshape(2, 2, half, HEAD_DIM, CMP_HIDDEN)
    w1s = jnp.einsum('ihldj,gk->ihlgdkj', w1r, eye).reshape(2, 2, half * KV_WIDTH, KV_GROUPS * CMP_HIDDEN)
    pos2 = jnp.broadcast_to(cmp_pos.reshape(2, 2, half, 1, HEAD_DIM), (2, 2, half, KV_GROUPS, HEAD_DIM))
    pos2 = pos2.reshape(2, 2, half * KV_WIDTH)
    b1s = jnp.tile(cmp_b1, (1, KV_GROUPS)).reshape(2, 1, KV_GROUPS * CMP_HIDDEN)
    w2s = jnp.einsum('ijd,gk->igjkd', cmp_w2, eye).reshape(2, KV_GROUPS * CMP_HIDDEN, KV_WIDTH)
    b2s = jnp.tile(cmp_b2, (1, KV_GROUPS)).reshape(2, 1, KV_WIDTH)
    return pos2, w1s.astype(BF16), b1s, w2s.astype(BF16), b2s


SEL_BLOCK = 64
N_SELECT = 16
N_LOCAL_SEL = 2
FORCED_SCORE = 1e9
WINDOW = 512
NEG_INF = -1e30
NSA_TQ = 128
SEL_CHUNK = 256
WIN_CHUNKS = 3
MASK_BIG = 2.0 ** 100


def _nsa_body(q_ref, qr_ref, gate_ref, kcmp_ref, vcmp_ref, ks_ref, vs_ref, kw_ref, vw_ref, ov_ref,
              o_ref, kaug_ref, s_ref, sw_ref, m_ref, l_ref, acc_ref, *, seq):
    tq = NSA_TQ
    rows = NSA_HEADS * tq
    n_blk = seq // SEL_BLOCK
    n_sel = min(N_SELECT, n_blk)
    qi = pl.program_id(1)
    q0 = qi * tq

    @pl.when(qi == 0)
    def _():
        kaug_ref[:, 0:LANES] = ks_ref[0]
        key_blk = lax.broadcasted_iota(jnp.int32, (seq, LANES), 0) // SEL_BLOCK
        lane = lax.broadcasted_iota(jnp.int32, (seq, LANES), 1)
        kaug_ref[:, LANES:2 * LANES] = jnp.where(key_blk == lane, 1.0, 0.0).astype(BF16)

    t_col = q0 + lax.broadcasted_iota(jnp.int32, (tq, LANES), 0)
    lane = lax.broadcasted_iota(jnp.int32, (tq, LANES), 1)

    cmp_ok = lane * CMP_STRIDE + (CMP_BLOCK - 1) <= t_col
    o_cmp = []
    psum = [jnp.zeros((tq, LANES), F32) for _ in range(KV_GROUPS)]
    for h in range(NSA_HEADS):
        qh = q_ref[0, :, h * LANES:(h + 1) * LANES]
        s = lax.dot_general(qh, kcmp_ref[0], (((1,), (1,)), ((), ())), preferred_element_type=F32)
        s = jnp.where(cmp_ok, s, NEG_INF)
        e = jnp.exp(s - jnp.max(s, axis=1, keepdims=True))
        p = jnp.where(cmp_ok, e / jnp.sum(e, axis=1, keepdims=True), 0.0)
        psum[h // HEADS_PER_GROUP] = psum[h // HEADS_PER_GROUP] + p
        o_cmp.append(jnp.dot(p.astype(BF16), vcmp_ref[0], preferred_element_type=F32))

    blk = lane % n_blk
    cur = t_col // SEL_BLOCK
    dist = cur - blk
    forced = (blk == 0) | ((dist >= 0) & (dist < N_LOCAL_SEL))
    aug = []
    for g in range(KV_GROUPS):
        hi = psum[g].astype(BF16)
        r1 = psum[g] - hi.astype(F32)
        mid = r1.astype(BF16)
        lo = (r1 - mid.astype(F32)).astype(BF16)
        imp = (jnp.dot(hi, ov_ref[...], preferred_element_type=F32)
               + jnp.dot(mid, ov_ref[...], preferred_element_type=F32)
               + jnp.dot(lo, ov_ref[...], preferred_element_type=F32))
        v = jnp.where(forced, FORCED_SCORE, jnp.where(dist >= 0, imp, -1.0))
        rank = jnp.zeros((tq, LANES), jnp.int32)
        for r in range(1, n_blk):
            w = pltpu.roll(v, r, 1)
            before = (w > v) | ((w == v) & (blk >= r))
            rank = rank + before.astype(jnp.int32)
        keep = (rank < n_sel) | (lane >= n_blk)
        aug.append(jnp.where(keep, 0.0, -MASK_BIG).astype(BF16))

    lhs = jnp.concatenate(
        [jnp.concatenate([qr_ref[0, :, h * LANES:(h + 1) * LANES], aug[h // HEADS_PER_GROUP]], axis=1)
         for h in range(NSA_HEADS)], axis=0)
    t_rows = q0 + lax.broadcasted_iota(jnp.int32, (rows, SEL_CHUNK), 0) % tq
    col = lax.broadcasted_iota(jnp.int32, (rows, SEL_CHUNK), 1)
    last = qi // (SEL_CHUNK // tq)

    def sel_scores(j):
        k0 = pl.multiple_of(j * SEL_CHUNK, SEL_CHUNK)
        return lax.dot_general(lhs, kaug_ref[pl.ds(k0, SEL_CHUNK), :], (((1,), (1,)), ((), ())),
                               preferred_element_type=F32)

    m_ref[...] = jnp.full((rows, LANES), NEG_INF, F32)

    def pass1(j, carry):
        s = sel_scores(j)
        s_ref[j] = s
        m_ref[...] = jnp.maximum(m_ref[...], jnp.maximum(s[:, :LANES], s[:, LANES:]))
        return carry

    lax.fori_loop(0, last, pass1, 0)
    s = jnp.where(last * SEL_CHUNK + col <= t_rows, sel_scores(last), NEG_INF)
    s_ref[last] = s
    m_ref[...] = jnp.maximum(m_ref[...], jnp.maximum(s[:, :LANES], s[:, LANES:]))
    m_ref[...] = jnp.broadcast_to(jnp.max(m_ref[...], axis=1, keepdims=True), (rows, LANES))

    l_ref[...] = jnp.zeros((rows, LANES), F32)
    acc_ref[...] = jnp.zeros((rows, LANES), F32)

    def pass2(j, carry):
        k0 = pl.multiple_of(j * SEL_CHUNK, SEL_CHUNK)
        m = m_ref[...]
        p = jnp.exp(s_ref[j] - jnp.concatenate([m, m], axis=1))
        l_ref[...] += p[:, :LANES] + p[:, LANES:]
        acc_ref[...] += jnp.dot(p.astype(BF16), vs_ref[0, pl.ds(k0, SEL_CHUNK), :], preferred_element_type=F32)
        return carry

    lax.fori_loop(0, last + 1, pass2, 0)
    o_slc = acc_ref[...] / jnp.sum(l_ref[...], axis=1, keepdims=True)

    lhs_w = jnp.concatenate([qr_ref[0, :, h * LANES:(h + 1) * LANES] for h in range(NSA_HEADS)], axis=0)
    w0 = pl.multiple_of(jnp.maximum(q0 + tq - WIN_CHUNKS * SEL_CHUNK, 0), tq)
    m = jnp.full((rows, LANES), NEG_INF, F32)
    for d in range(WIN_CHUNKS):
        k0 = pl.multiple_of(w0 + d * SEL_CHUNK, tq)
        s = lax.dot_general(lhs_w, kw_ref[0, pl.ds(k0, SEL_CHUNK), :], (((1,), (1,)), ((), ())),
                            preferred_element_type=F32)
        key = k0 + col
        s = jnp.where((key <= t_rows) & (key > t_rows - WINDOW), s, NEG_INF)
        sw_ref[d] = s
        m = jnp.maximum(m, jnp.maximum(s[:, :LANES], s[:, LANES:]))
    m = jnp.broadcast_to(jnp.max(m, axis=1, keepdims=True), (rows, LANES))
    m2 = jnp.concatenate([m, m], axis=1)
    lw = jnp.zeros((rows, LANES), F32)
    accw = jnp.zeros((rows, LANES), F32)
    for d in range(WIN_CHUNKS):
        k0 = pl.multiple_of(w0 + d * SEL_CHUNK, tq)
        p = jnp.exp(sw_ref[d] - m2)
        lw = lw + p[:, :LANES] + p[:, LANES:]
        accw = accw + jnp.dot(p.astype(BF16), vw_ref[0, pl.ds(k0, SEL_CHUNK), :], preferred_element_type=F32)
    o_win = accw / jnp.sum(lw, axis=1, keepdims=True)

    res = []
    for h in range(NSA_HEADS):
        r = slice(h * tq, (h + 1) * tq)
        g = gate_ref[0]
        res.append(g[:, 3 * h:3 * h + 1] * o_cmp[h] + g[:, 3 * h + 1:3 * h + 2] * o_slc[r]
                   + g[:, 3 * h + 2:3 * h + 3] * o_win[r])
    for j in range(NSA_HEADS // 2):
        a, b = res[2 * j], res[2 * j + 1]
        if (2 * j) // HEADS_PER_GROUP == 0:
            b = pltpu.roll(b, HEAD_DIM, 1)
        else:
            a = pltpu.roll(a, HEAD_DIM, 1)
        o_ref[0, :, j * LANES:(j + 1) * LANES] = jnp.where(lane < HEAD_DIM, a, b)


def _nsa(q3, qr3, gate3, kcmp, vcmp, ks3, vs3, kw3, vw3, ov_rep):
    bsz, seq, qw = q3.shape
    assert seq % SEL_CHUNK == 0 and seq >= WIN_CHUNKS * SEL_CHUNK and LANES % (seq // SEL_BLOCK) == 0
    assert seq // CMP_STRIDE == LANES, "compressed tokens are indexed by lane"
    tq = NSA_TQ
    rows = NSA_HEADS * tq
    tile = lambda w: pl.BlockSpec((1, tq, w), lambda bi, qi: (bi, qi, 0))
    per_b = lambda a: pl.BlockSpec((1,) + a.shape[1:], lambda bi, qi: (bi, 0, 0))
    return pl.pallas_call(
        functools.partial(_nsa_body, seq=seq),
        grid=(bsz, seq // tq),
        in_specs=[tile(qw), tile(qw), tile(LANES), per_b(kcmp), per_b(vcmp), per_b(ks3), per_b(vs3),
                  per_b(kw3), per_b(vw3), pl.BlockSpec(ov_rep.shape, lambda bi, qi: (0, 0))],
        out_specs=tile(NSA_HEADS * HEAD_DIM),
        out_shape=jax.ShapeDtypeStruct((bsz, seq, NSA_HEADS * HEAD_DIM), F32),
        scratch_shapes=[
            pltpu.VMEM((seq, 2 * LANES), BF16),
            pltpu.VMEM((seq // SEL_CHUNK, rows, SEL_CHUNK), F32),
            pltpu.VMEM((WIN_CHUNKS, rows, SEL_CHUNK), F32),
            pltpu.VMEM((rows, LANES), F32),
            pltpu.VMEM((rows, LANES), F32),
            pltpu.VMEM((rows, LANES), F32),
        ],
        compiler_params=pltpu.CompilerParams(dimension_semantics=("arbitrary", "arbitrary"), vmem_limit_bytes=VMEM_LIMIT),
        name="nsa",
    )(q3, qr3, gate3, kcmp, vcmp, ks3, vs3, kw3, vw3, ov_rep)


MEM_HEADS = 4
MEM_HEAD_DIM = D_MODEL // MEM_HEADS


def _memkv_body(mem_ref, g_ref, wk_ref, wv_ref, k_ref, v_ref):
    mb = _rms(mem_ref[0], g_ref[...]).astype(BF16)
    k_ref[0] = jnp.dot(mb, wk_ref[...], preferred_element_type=F32).astype(BF16)
    v_ref[0] = jnp.dot(mb, wv_ref[...], preferred_element_type=F32).astype(BF16)


def _memkv(mem, g, wk, wv):
    bsz, mlen, _ = mem.shape
    full = lambda a: pl.BlockSpec(a.shape, lambda bi: (0,) * a.ndim)
    blk = pl.BlockSpec((1, mlen, D_MODEL), lambda bi: (bi, 0, 0))
    return pl.pallas_call(
        _memkv_body,
        grid=(bsz,),
        in_specs=[blk, full(g), full(wk), full(wv)],
        out_specs=[blk, blk],
        out_shape=[jax.ShapeDtypeStruct((bsz, mlen, D_MODEL), BF16)] * 2,
        compiler_params=pltpu.CompilerParams(dimension_semantics=("arbitrary",), vmem_limit_bytes=VMEM_LIMIT),
        name="memkv",
    )(mem, g, wk, wv)


def _mid_body(x_ref, yc_ref, yn_ref, wo_ref, gq_ref, wq_ref, km_ref, vm_ref, wmo_ref, gp_ref, h_ref, hp_ref):
    h1 = (x_ref[0]
          + jnp.dot(yc_ref[0].astype(BF16), wo_ref[0:CONV_WIDTH, :], preferred_element_type=F32)
          + jnp.dot(yn_ref[0].astype(BF16), wo_ref[CONV_WIDTH:, :], preferred_element_type=F32))
    qm = jnp.dot(_rms(h1, gq_ref[...]).astype(BF16), wq_ref[...], preferred_element_type=F32)
    heads = []
    for hh in range(MEM_HEADS):
        c = slice(hh * MEM_HEAD_DIM, (hh + 1) * MEM_HEAD_DIM)
        s = lax.dot_general(qm[:, c].astype(BF16), km_ref[0, :, c], (((1,), (1,)), ((), ())),
                            preferred_element_type=F32) * (MEM_HEAD_DIM ** -0.5)
        e = jnp.exp(s - jnp.max(s, axis=1, keepdims=True))
        p = e / jnp.sum(e, axis=1, keepdims=True)
        heads.append(jnp.dot(p.astype(BF16), vm_ref[0, :, c], preferred_element_type=F32))
    o = jnp.concatenate(heads, axis=1)
    h2 = h1 + jnp.dot(o.astype(BF16), wmo_ref[...], preferred_element_type=F32)
    h_ref[0] = h2
    hp_ref[0] = _rms(h2, gp_ref[...])


def _mid(x3, yc3, yn3, wo, gq, wq, km, vm, wmo, gp, tm=256):
    bsz, seq, _ = x3.shape
    full = lambda a: pl.BlockSpec(a.shape, lambda bi, ti: (0,) * a.ndim)
    tile = lambda w: pl.BlockSpec((1, tm, w), lambda bi, ti: (bi, ti, 0))
    per_b = lambda a: pl.BlockSpec((1,) + a.shape[1:], lambda bi, ti: (bi, 0, 0))
    out = jax.ShapeDtypeStruct((bsz, seq, D_MODEL), F32)
    return pl.pallas_call(
        _mid_body,
        grid=(bsz, seq // tm),
        in_specs=[tile(D_MODEL), tile(CONV_WIDTH), tile(CONV_WIDTH), full(wo), full(gq), full(wq),
                  per_b(km), per_b(vm), full(wmo), full(gp)],
        out_specs=[tile(D_MODEL), tile(D_MODEL)],
        out_shape=[out, out],
        compiler_params=pltpu.CompilerParams(dimension_semantics=("arbitrary", "arbitrary"), vmem_limit_bytes=VMEM_LIMIT),
        name="mid",
    )(x3, yc3, yn3, wo, gq, wq, km, vm, wmo, gp)


PEER_HEADS = 8
PEER_KEYS = 128
PEER_HALF = 128
PEER_TOPK = 16
PEER_SLOTS = PEER_HEADS * PEER_TOPK


def _topk_rows(s, k, payload=None):
    nrow, tm = s.shape
    riota = lax.broadcasted_iota(jnp.int32, (nrow, tm), 0)
    kiota = lax.broadcasted_iota(jnp.int32, (k, tm), 0)
    vals = jnp.zeros((k, tm), F32)
    picks = jnp.zeros((k, tm), jnp.int32)
    for it in range(k):
        m = jnp.max(s, axis=0, keepdims=True)
        idx = jnp.min(jnp.where(s == m, riota, nrow), axis=0, keepdims=True)
        hit = riota == idx
        pick = idx if payload is None else jnp.sum(jnp.where(hit, payload, 0), axis=0, keepdims=True)
        vals = jnp.where(kiota == it, m, vals)
        picks = jnp.where(kiota == it, pick, picks)
        s = jnp.where(hit, -jnp.inf, s)
    return vals, picks


def _peer_topk_body(hp_ref, wqt_ref, keys_ref, ex_ref, gate_ref):
    hb = hp_ref[...].astype(BF16)
    qt = lax.dot_general(wqt_ref[...], hb, (((1,), (1,)), ((), ())), preferred_element_type=F32)
    for h in range(PEER_HEADS):
        top = []
        for p in range(2):
            c = 2 * h + p
            qc = qt[c * PEER_HALF:(c + 1) * PEER_HALF, :].astype(BF16)
            s = jnp.dot(keys_ref[c], qc, preferred_element_type=F32)
            top.append(_topk_rows(s, PEER_TOPK))
        (v0, i0), (v1, i1) = top
        cand = jnp.concatenate([v0[i:i + 1, :] + v1 for i in range(PEER_TOPK)], axis=0)
        expert = jnp.concatenate([i0[i:i + 1, :] * PEER_KEYS + i1 for i in range(PEER_TOPK)], axis=0)
        best, ex = _topk_rows(cand, PEER_TOPK, payload=expert)
        e = jnp.exp(best - jnp.max(best, axis=0, keepdims=True))
        r = slice(h * PEER_TOPK, (h + 1) * PEER_TOPK)
        ex_ref[0, r, :] = ex
        gate_ref[0, r, :] = e / jnp.sum(e, axis=0, keepdims=True)


def _peer_topk(hp2, wqt, keys, tm=128):
    n = hp2.shape[0]
    full = lambda a: pl.BlockSpec(a.shape, lambda i: (0,) * a.ndim)
    out = pl.BlockSpec((1, PEER_SLOTS, tm), lambda i: (i, 0, 0))
    return pl.pallas_call(
        _peer_topk_body,
        grid=(n // tm,),
        in_specs=[pl.BlockSpec((tm, D_MODEL), lambda i: (i, 0)), full(wqt), full(keys)],
        out_specs=[out, out],
        out_shape=[jax.ShapeDtypeStruct((n // tm, PEER_SLOTS, tm), jnp.int32),
                   jax.ShapeDtypeStruct((n // tm, PEER_SLOTS, tm), F32)],
        compiler_params=pltpu.CompilerParams(dimension_semantics=("arbitrary",), vmem_limit_bytes=VMEM_LIMIT),
        name="peer_topk",
    )(hp2, wqt, keys)


PEER_TG = 8


def _peer_body(idx_hbm, u_hbm, v_hbm, hp_ref, gate_ref, h_ref, gf_ref, o_ref,
               idx_smem, ubuf, vbuf, sem_i, sem_u, sem_v, *, nsteps):
    i = pl.program_id(0)
    tg = PEER_TG
    per = tg * PEER_SLOTS
    slot = i % 2
    nxt = 1 - slot

    def idx_copy(step, s):
        return pltpu.make_async_copy(idx_hbm.at[pl.ds(step * per, per)], idx_smem.at[s], sem_i.at[s])

    def issue_rows(s):
        def body(j, carry):
            e = idx_smem[s, j]
            n = j // PEER_SLOTS
            k = j % PEER_SLOTS
            pltpu.make_async_copy(u_hbm.at[pl.ds(e, 1), :], ubuf.at[s, n, pl.ds(k, 1), :], sem_u.at[s]).start()
            pltpu.make_async_copy(v_hbm.at[pl.ds(e, 1), :], vbuf.at[s, n, pl.ds(k, 1), :], sem_v.at[s]).start()
            return carry
        lax.fori_loop(0, per, body, 0, unroll=8)

    @pl.when(i == 0)
    def _():
        idx_copy(0, 0).start()
        idx_copy(0, 0).wait()
        issue_rows(0)

        @pl.when(nsteps > 1)
        def _():
            idx_copy(1, 1).start()

    @pl.when(i + 1 < nsteps)
    def _():
        idx_copy(i + 1, nxt).wait()
        issue_rows(nxt)

        @pl.when(i + 2 < nsteps)
        def _():
            idx_copy(i + 2, slot).start()

    pltpu.make_async_copy(ubuf.at[slot], ubuf.at[slot], sem_u.at[slot]).wait()
    pltpu.make_async_copy(vbuf.at[slot], vbuf.at[slot], sem_v.at[slot]).wait()

    rows = []
    for n in range(tg):
        xr = jnp.broadcast_to(hp_ref[n:n + 1, :], (SUBLANES, D_MODEL)).astype(BF16)
        a = lax.dot_general(xr, ubuf[slot, n].astype(BF16), (((1,), (1,)), ((), ())), preferred_element_type=F32)
        ga = (jax.nn.gelu(a) * gate_ref[n:n + 1, :]).astype(BF16)
        y = jnp.dot(ga, vbuf[slot, n].astype(BF16), preferred_element_type=F32)
        rows.append(y[0:1, :])
    o_ref[...] = _rms(h_ref[...] + jnp.concatenate(rows, axis=0), gf_ref[...])


def _peer(idx_flat, u, v, hp2, gates2, h2, gf):
    n = hp2.shape[0]
    tg = PEER_TG
    nsteps = n // tg
    tile = lambda w: pl.BlockSpec((tg, w), lambda i: (i, 0))
    anyspace = pl.BlockSpec(memory_space=pl.ANY)
    return pl.pallas_call(
        functools.partial(_peer_body, nsteps=nsteps),
        grid=(nsteps,),
        in_specs=[anyspace, anyspace, anyspace, tile(D_MODEL), tile(PEER_SLOTS), tile(D_MODEL),
                  pl.BlockSpec(gf.shape, lambda i: (0, 0))],
        out_specs=tile(D_MODEL),
        out_shape=jax.ShapeDtypeStruct((n, D_MODEL), F32),
        scratch_shapes=[
            pltpu.SMEM((2, tg * PEER_SLOTS), jnp.int32),
            pltpu.VMEM((2, tg, PEER_SLOTS, D_MODEL), F32),
            pltpu.VMEM((2, tg, PEER_SLOTS, D_MODEL), F32),
            pltpu.SemaphoreType.DMA((2,)),
            pltpu.SemaphoreType.DMA((2,)),
            pltpu.SemaphoreType.DMA((2,)),
        ],
        compiler_params=pltpu.CompilerParams(dimension_semantics=("arbitrary",), vmem_limit_bytes=VMEM_LIMIT),
        name="peer",
    )(idx_flat, u, v, hp2, gates2, h2, gf)


def kernel(x, mem, positions, mix_norm_g, w_in, conv_dw_w, conv_dw_b, conv_ln_g, conv_ln_b, cmp_pos, cmp_w1, cmp_b1,
           cmp_w2, cmp_b2, w_out, mem_q_norm_g, mem_kv_norm_g, w_mem_q, w_mem_k, w_mem_v, w_mem_o, peer_norm_g,
           peer_w_q, peer_sub_keys, peer_u, peer_v, final_norm_g):
    bsz, seq, d = x.shape
    n = bsz * seq
    depth = w_in.shape[0]
    assert depth == 1, "the final rmsnorm is fused into the last layer's PEER kernel"
    row = lambda a: a.reshape(1, -1)
    l = 0
    wab, wq_pad, wkv, wg_pad, inv_lane, posb = _prep_inproj(w_in[l], positions)
    u, q, qr, kc, vc, ks, vs, kw, vw, gate = _inproj(
        x.reshape(n, d), posb, row(mix_norm_g[l]), inv_lane, wab, wq_pad, wkv, wg_pad)
    y_conv = _conv(u.reshape(bsz, seq, CONV_WIDTH), conv_dw_w[l], row(conv_dw_b[l]), row(conv_ln_g[l]),
                   row(conv_ln_b[l]))
    stride_rows = lambda a: a.reshape(bsz, seq // CMP_STRIDE, CMP_STRIDE * KV_WIDTH)
    kcmp, vcmp = _compress(stride_rows(kc), stride_rows(vc),
                           *_prep_compress(cmp_pos[l], cmp_w1[l], cmp_b1[l], cmp_w2[l], cmp_b2[l]))
    b3 = lambda a: a.reshape(bsz, seq, a.shape[-1])
    y_nsa = _nsa(b3(q), b3(qr), b3(gate), kcmp, vcmp, b3(ks), b3(vs), b3(kw), b3(vw), _overlap_table(seq))
    km, vm = _memkv(mem, row(mem_kv_norm_g[l]), w_mem_k[l].astype(BF16), w_mem_v[l].astype(BF16))
    h2, hp = _mid(x, y_conv, y_nsa, w_out[l].astype(BF16), row(mem_q_norm_g[l]), w_mem_q[l].astype(BF16), km, vm,
                  w_mem_o[l].astype(BF16), row(peer_norm_g[l]))
    keys = peer_sub_keys[l].reshape(2 * PEER_HEADS, PEER_KEYS, PEER_HALF).astype(BF16)
    ex_t, gate_t = _peer_topk(hp.reshape(n, d), peer_w_q[l].T.astype(BF16), keys)
    token_major = lambda a: jnp.swapaxes(a, 1, 2).reshape(n, PEER_SLOTS)
    out = _peer(token_major(ex_t).reshape(n * PEER_SLOTS), peer_u[l], peer_v[l], hp.reshape(n, d),
                token_major(gate_t), h2.reshape(n, d), row(final_norm_g))
    return out.reshape(bsz, seq, d)


def _overlap_table(seq):
    n_blk = seq // SEL_BLOCK
    n_cmp = (seq - CMP_BLOCK) // CMP_STRIDE + 1
    c = jnp.arange(LANES)[:, None]
    s = (jnp.arange(LANES) % n_blk)[None, :]
    ov = ((c * CMP_STRIDE <= s * SEL_BLOCK + SEL_BLOCK - 1) & (c * CMP_STRIDE + CMP_BLOCK - 1 >= s * SEL_BLOCK)
          & (c < n_cmp))
    return ov.astype(BF16)
```

```python
import functools

import jax
import jax.numpy as jnp
from jax import lax
from jax.experimental import pallas as pl
from jax.experimental.pallas import tpu as pltpu

F32 = jnp.float32
BF16 = jnp.bfloat16

D_MODEL = 1024
EPS = 1e-6
CONV_WIDTH = 512
CONV_KERNEL = 31
NSA_HEADS = 8
HEAD_DIM = 64
KV_GROUPS = 2
HEADS_PER_GROUP = NSA_HEADS // KV_GROUPS
KV_WIDTH = KV_GROUPS * HEAD_DIM
ROT_DIM = HEAD_DIM // 4
ROT_HALF = ROT_DIM // 2
ROPE_THETA = 500000.0
LANES = 128
VMEM_LIMIT = 56 * 1024 * 1024


def _rms(x, g):
    return (x * lax.rsqrt(jnp.mean(x * x, axis=-1, keepdims=True) + EPS)) * g


def _rope_blocks(v, c, s1, s2):
    outs = []
    for j in range(v.shape[1] // LANES):
        blk = v[:, j * LANES:(j + 1) * LANES]
        outs.append(blk * c + pltpu.roll(blk, LANES - ROT_HALF, 1) * s1 + pltpu.roll(blk, ROT_HALF, 1) * s2)
    return outs


def _inproj_body(x_ref, posb_ref, g_ref, inv_ref, wab_ref, wq_ref, wkv_ref, wg_ref,
                 u_ref, q_ref, qr_ref, kc_ref, vc_ref, ks_ref, vs_ref, kw_ref, vw_ref, gate_ref):
    hb = _rms(x_ref[...], g_ref[...]).astype(BF16)
    ab = jnp.dot(hb, wab_ref[...], preferred_element_type=F32)
    u_ref[...] = ab[:, :CONV_WIDTH] * jax.nn.sigmoid(ab[:, CONV_WIDTH:])

    ang = posb_ref[...] * inv_ref[...]
    c = jnp.cos(ang)
    s = jnp.sin(ang)
    lane = lax.broadcasted_iota(jnp.int32, (1, LANES), 1) % HEAD_DIM
    s1 = jnp.where(lane < ROT_HALF, -s, 0.0)
    s2 = jnp.where((lane >= ROT_HALF) & (lane < ROT_DIM), s, 0.0)

    q = jnp.dot(hb, wq_ref[...], preferred_element_type=F32) * (HEAD_DIM ** -0.5)
    q_ref[...] = q.astype(BF16)
    for j, blk in enumerate(_rope_blocks(q, c, s1, s2)):
        qr_ref[:, j * LANES:(j + 1) * LANES] = blk.astype(BF16)

    kv = jnp.dot(hb, wkv_ref[...], preferred_element_type=F32)
    kc_ref[...] = kv[:, 0 * KV_WIDTH:1 * KV_WIDTH]
    vc_ref[...] = kv[:, 1 * KV_WIDTH:2 * KV_WIDTH]
    ks_ref[...] = _rope_blocks(kv[:, 2 * KV_WIDTH:3 * KV_WIDTH], c, s1, s2)[0].astype(BF16)
    vs_ref[...] = kv[:, 3 * KV_WIDTH:4 * KV_WIDTH].astype(BF16)
    kw_ref[...] = _rope_blocks(kv[:, 4 * KV_WIDTH:5 * KV_WIDTH], c, s1, s2)[0].astype(BF16)
    vw_ref[...] = kv[:, 5 * KV_WIDTH:6 * KV_WIDTH].astype(BF16)

    gate_ref[...] = jax.nn.sigmoid(jnp.dot(hb, wg_ref[...], preferred_element_type=F32))


def _inproj(x2, posb, g, inv_lane, wab, wq_pad, wkv, wg_pad, tm=256):
    n = x2.shape[0]
    row = lambda w: pl.BlockSpec((tm, w), lambda i: (i, 0))
    full = lambda a: pl.BlockSpec(a.shape, lambda i: (0,) * a.ndim)
    qw = NSA_HEADS * LANES
    out_shape = [
        jax.ShapeDtypeStruct((n, CONV_WIDTH), F32),
        jax.ShapeDtypeStruct((n, qw), BF16),
        jax.ShapeDtypeStruct((n, qw), BF16),
        jax.ShapeDtypeStruct((n, KV_WIDTH), F32),
        jax.ShapeDtypeStruct((n, KV_WIDTH), F32),
        jax.ShapeDtypeStruct((n, KV_WIDTH), BF16),
        jax.ShapeDtypeStruct((n, KV_WIDTH), BF16),
        jax.ShapeDtypeStruct((n, KV_WIDTH), BF16),
        jax.ShapeDtypeStruct((n, KV_WIDTH), BF16),
        jax.ShapeDtypeStruct((n, LANES), F32),
    ]
    return pl.pallas_call(
        _inproj_body,
        grid=(n // tm,),
        in_specs=[row(D_MODEL), row(LANES), full(g), full(inv_lane), full(wab), full(wq_pad), full(wkv), full(wg_pad)],
        out_specs=[row(s.shape[1]) for s in out_shape],
        out_shape=out_shape,
        compiler_params=pltpu.CompilerParams(dimension_semantics=("arbitrary",), vmem_limit_bytes=VMEM_LIMIT),
        name="inproj",
    )(x2, posb, g, inv_lane, wab, wq_pad, wkv, wg_pad)


def _prep_inproj(w_in_l, positions):
    c2 = 2 * CONV_WIDTH
    wab = w_in_l[:, :c2].astype(BF16)
    wq = w_in_l[:, c2:c2 + NSA_HEADS * HEAD_DIM].reshape(D_MODEL, NSA_HEADS, HEAD_DIM)
    z = jnp.zeros_like(wq[:, :HEADS_PER_GROUP])
    wq_pad = jnp.concatenate([
        jnp.concatenate([wq[:, :HEADS_PER_GROUP], z], axis=-1),
        jnp.concatenate([z, wq[:, HEADS_PER_GROUP:]], axis=-1)], axis=1)
    wq_pad = wq_pad.reshape(D_MODEL, NSA_HEADS * LANES).astype(BF16)
    k0 = c2 + NSA_HEADS * HEAD_DIM
    wkv = w_in_l[:, k0:k0 + 6 * KV_WIDTH].astype(BF16)
    wg = w_in_l[:, k0 + 6 * KV_WIDTH:]
    wg_pad = jnp.pad(wg, ((0, 0), (0, LANES - wg.shape[1]))).astype(BF16)
    inv = ROPE_THETA ** (-(jnp.arange(ROT_HALF, dtype=F32) * 2.0 / ROT_DIM))
    lane = jnp.arange(LANES)
    inv_lane = jnp.where(lane % HEAD_DIM < ROT_DIM, inv[lane % ROT_HALF], 0.0).reshape(1, LANES).astype(F32)
    n = positions.size
    posb = jnp.broadcast_to(positions.reshape(n, 1).astype(F32), (n, LANES))
    return wab, wq_pad, wkv, wg_pad, inv_lane, posb


CONV_TOP = 32
CONV_TAIL = 8
SUBLANES = 8


def _conv_body(u_ref, w_ref, b_ref, lg_ref, lb_ref, o_ref, pad_ref, *, tt, seq):
    ti = pl.program_id(1)

    @pl.when(ti == 0)
    def _():
        pad_ref[0:CONV_TOP, :] = jnp.zeros((CONV_TOP, CONV_WIDTH), F32)
        pad_ref[CONV_TOP:CONV_TOP + seq, :] = u_ref[0]
        pad_ref[CONV_TOP + seq:, :] = jnp.zeros((CONV_TAIL, CONV_WIDTH), F32)

    r0 = pl.multiple_of(ti * tt, tt)
    win = pad_ref[pl.ds(r0, tt + CONV_TOP + CONV_TAIL), :]
    first = CONV_TOP - (CONV_KERNEL - 1)
    acc = jnp.zeros((tt, CONV_WIDTH), F32)
    for r in range(SUBLANES):
        taps = [k for k in range(CONV_KERNEL) if (first + k) % SUBLANES == r]
        if not taps:
            continue
        sh = win[r:r + tt + CONV_TOP, :]
        for k in taps:
            a = (first + k) // SUBLANES * SUBLANES
            acc = acc + sh[a:a + tt, :] * w_ref[k:k + 1, :]
    y = acc + b_ref[...]
    mu = jnp.mean(y, axis=-1, keepdims=True)
    yc = y - mu
    var = jnp.mean(yc * yc, axis=-1, keepdims=True)
    z = yc * lax.rsqrt(var + EPS) * lg_ref[...] + lb_ref[...]
    o_ref[0] = z * jax.nn.sigmoid(z)


def _conv(u3, w, b, lg, lb, tt=64):
    bsz, seq, _ = u3.shape
    full = lambda a: pl.BlockSpec(a.shape, lambda bi, ti: (0,) * a.ndim)
    return pl.pallas_call(
        functools.partial(_conv_body, tt=tt, seq=seq),
        grid=(bsz, seq // tt),
        in_specs=[pl.BlockSpec((1, seq, CONV_WIDTH), lambda bi, ti: (bi, 0, 0)), full(w), full(b), full(lg), full(lb)],
        out_specs=pl.BlockSpec((1, tt, CONV_WIDTH), lambda bi, ti: (bi, ti, 0)),
        out_shape=jax.ShapeDtypeStruct((bsz, seq, CONV_WIDTH), F32),
        scratch_shapes=[pltpu.VMEM((CONV_TOP + seq + CONV_TAIL, CONV_WIDTH), F32)],
        compiler_params=pltpu.CompilerParams(dimension_semantics=("arbitrary", "arbitrary"), vmem_limit_bytes=VMEM_LIMIT),
        name="conv",
    )(u3, w, b, lg, lb)


CMP_BLOCK = 32
CMP_STRIDE = 16
CMP_HIDDEN = 128


def _compress_body(kc_ref, vc_ref, pos_ref, w1_ref, b1_ref, w2_ref, b2_ref, ko_ref, vo_ref):
    for i, (src, dst) in enumerate(((kc_ref, ko_ref), (vc_ref, vo_ref))):
        rows = src[0]
        nrow = rows.shape[0]
        top = jnp.dot((rows + pos_ref[i, 0:1, :]).astype(BF16), w1_ref[i, 0], preferred_element_type=F32)
        bot = jnp.dot((rows + pos_ref[i, 1:2, :]).astype(BF16), w1_ref[i, 1], preferred_element_type=F32)
        h = top + pltpu.roll(bot, nrow - 1, 0) + b1_ref[i]
        out = jnp.dot(jax.nn.gelu(h).astype(BF16), w2_ref[i], preferred_element_type=F32) + b2_ref[i]
        valid = lax.broadcasted_iota(jnp.int32, out.shape, 0) < nrow - 1
        dst[0] = jnp.where(valid, out, 0.0).astype(BF16)


def _compress(kc3, vc3, pos2, w1s, b1s, w2s, b2s):
    bsz, nrow, width = kc3.shape
    full = lambda a: pl.BlockSpec(a.shape, lambda bi: (0,) * a.ndim)
    row = pl.BlockSpec((1, nrow, width), lambda bi: (bi, 0, 0))
    out = pl.BlockSpec((1, nrow, KV_WIDTH), lambda bi: (bi, 0, 0))
    return pl.pallas_call(
        _compress_body,
        grid=(bsz,),
        in_specs=[row, row, full(pos2), full(w1s), full(b1s), full(w2s), full(b2s)],
        out_specs=[out, out],
        out_shape=[jax.ShapeDtypeStruct((bsz, nrow, KV_WIDTH), BF16)] * 2,
        compiler_params=pltpu.CompilerParams(dimension_semantics=("arbitrary",), vmem_limit_bytes=VMEM_LIMIT),
        name="compress",
    )(kc3, vc3, pos2, w1s, b1s, w2s, b2s)


def _prep_compress(cmp_pos, cmp_w1, cmp_b1, cmp_w2, cmp_b2):
    half = CMP_BLOCK // 2
    eye = jnp.eye(KV_GROUPS, dtype=F32)
    w1r = cmp_w1.reshape(2, 2, half, HEAD_DIM, CMP_HIDDEN)
    w1s = jnp.einsum('ihldj,gk->ihlgdkj', w1r, eye).reshape(2, 2, half * KV_WIDTH, KV_GROUPS * CMP_HIDDEN)
    pos2 = jnp.broadcast_to(cmp_pos.reshape(2, 2, half, 1, HEAD_DIM), (2, 2, half, KV_GROUPS, HEAD_DIM))
    pos2 = pos2.reshape(2, 2, half * KV_WIDTH)
    b1s = jnp.tile(cmp_b1, (1, KV_GROUPS)).reshape(2, 1, KV_GROUPS * CMP_HIDDEN)
    w2s = jnp.einsum('ijd,gk->igjkd', cmp_w2, eye).reshape(2, KV_GROUPS * CMP_HIDDEN, KV_WIDTH)
    b2s = jnp.tile(cmp_b2, (1, KV_GROUPS)).reshape(2, 1, KV_WIDTH)
    return pos2, w1s.astype(BF16), b1s, w2s.astype(BF16), b2s


SEL_BLOCK = 64
N_SELECT = 16
N_LOCAL_SEL = 2
FORCED_SCORE = 1e9
WINDOW = 512
NEG_INF = -1e30
NSA_TQ = 128
SEL_CHUNK = 256
WIN_CHUNKS = 3
MASK_BIG = 2.0 ** 100


def _nsa_body(q_ref, qr_ref, gate_ref, kcmp_ref, vcmp_ref, ks_ref, vs_ref, kw_ref, vw_ref, ov_ref,
              o_ref, kaug_ref, s_ref, sw_ref, m_ref, l_ref, acc_ref, *, seq):
    tq = NSA_TQ
    rows = NSA_HEADS * tq
    n_blk = seq // SEL_BLOCK
    n_sel = min(N_SELECT, n_blk)
    qi = pl.program_id(1)
    q0 = qi * tq

    @pl.when(qi == 0)
    def _():
        kaug_ref[:, 0:LANES] = ks_ref[0]
        key_blk = lax.broadcasted_iota(jnp.int32, (seq, LANES), 0) // SEL_BLOCK
        lane = lax.broadcasted_iota(jnp.int32, (seq, LANES), 1)
        kaug_ref[:, LANES:2 * LANES] = jnp.where(key_blk == lane, 1.0, 0.0).astype(BF16)

    t_col = q0 + lax.broadcasted_iota(jnp.int32, (tq, LANES), 0)
    lane = lax.broadcasted_iota(jnp.int32, (tq, LANES), 1)

    cmp_ok = lane * CMP_STRIDE + (CMP_BLOCK - 1) <= t_col
    o_cmp = []
    psum = [jnp.zeros((tq, LANES), F32) for _ in range(KV_GROUPS)]
    for h in range(NSA_HEADS):
        qh = q_ref[0, :, h * LANES:(h + 1) * LANES]
        s = lax.dot_general(qh, kcmp_ref[0], (((1,), (1,)), ((), ())), preferred_element_type=F32)
        s = jnp.where(cmp_ok, s, NEG_INF)
        e = jnp.exp(s - jnp.max(s, axis=1, keepdims=True))
        p = jnp.where(cmp_ok, e / jnp.sum(e, axis=1, keepdims=True), 0.0)
        psum[h // HEADS_PER_GROUP] = psum[h // HEADS_PER_GROUP] + p
        o_cmp.append(jnp.dot(p.astype(BF16), vcmp_ref[0], preferred_element_type=F32))

    blk = lane % n_blk
    cur = t_col // SEL_BLOCK
    dist = cur - blk
    forced = (blk == 0) | ((dist >= 0) & (dist < N_LOCAL_SEL))
    aug = []
    for g in range(KV_GROUPS):
        hi = psum[g].astype(BF16)
        r1 = psum[g] - hi.astype(F32)
        mid = r1.astype(BF16)
        lo = (r1 - mid.astype(F32)).astype(BF16)
        imp = (jnp.dot(hi, ov_ref[...], preferred_element_type=F32)
               + jnp.dot(mid, ov_ref[...], preferred_element_type=F32)
               + jnp.dot(lo, ov_ref[...], preferred_element_type=F32))
        v = jnp.where(forced, FORCED_SCORE, jnp.where(dist >= 0, imp, -1.0))
        rank = jnp.zeros((tq, LANES), jnp.int32)
        for r in range(1, n_blk):
            w = pltpu.roll(v, r, 1)
            before = (w > v) | ((w == v) & (blk >= r))
            rank = rank + before.astype(jnp.int32)
        keep = (rank < n_sel) | (lane >= n_blk)
        aug.append(jnp.where(keep, 0.0, -MASK_BIG).astype(BF16))

    lhs = jnp.concatenate(
        [jnp.concatenate([qr_ref[0, :, h * LANES:(h + 1) * LANES], aug[h // HEADS_PER_GROUP]], axis=1)
         for h in range(NSA_HEADS)], axis=0)
    t_rows = q0 + lax.broadcasted_iota(jnp.int32, (rows, SEL_CHUNK), 0) % tq
    col = lax.broadcasted_iota(jnp.int32, (rows, SEL_CHUNK), 1)
    last = qi // (SEL_CHUNK // tq)

    def sel_scores(j):
        k0 = pl.multiple_of(j * SEL_CHUNK, SEL_CHUNK)
        return lax.dot_general(lhs, kaug_ref[pl.ds(k0, SEL_CHUNK), :], (((1,), (1,)), ((), ())),
                               preferred_element_type=F32)

    m_ref[...] = jnp.full((rows, LANES), NEG_INF, F32)

    def pass1(j, carry):
        s = sel_scores(j)
        s_ref[j] = s
        m_ref[...] = jnp.maximum(m_ref[...], jnp.maximum(s[:, :LANES], s[:, LANES:]))
        return carry

    lax.fori_loop(0, last, pass1, 0)
    s = jnp.where(last * SEL_CHUNK + col <= t_rows, sel_scores(last), NEG_INF)
    s_ref[last] = s
    m_ref[...] = jnp.maximum(m_ref[...], jnp.maximum(s[:, :LANES], s[:, LANES:]))
    m_ref[...] = jnp.broadcast_to(jnp.max(m_ref[...], axis=1, keepdims=True), (rows, LANES))

    l_ref[...] = jnp.zeros((rows, LANES), F32)
    acc_ref[...] = jnp.zeros((rows, LANES), F32)

    def pass2(j, carry):
        k0 = pl.multiple_of(j * SEL_CHUNK, SEL_CHUNK)
        m = m_ref[...]
        p = jnp.exp(s_ref[j] - jnp.concatenate([m, m], axis=1))
        l_ref[...] += p[:, :LANES] + p[:, LANES:]
        acc_ref[...] += jnp.dot(p.astype(BF16), vs_ref[0, pl.ds(k0, SEL_CHUNK), :], preferred_element_type=F32)
        return carry

    lax.fori_loop(0, last + 1, pass2, 0)
    o_slc = acc_ref[...] / jnp.sum(l_ref[...], axis=1, keepdims=True)

    lhs_w = jnp.concatenate([qr_ref[0, :, h * LANES:(h + 1) * LANES] for h in range(NSA_HEADS)], axis=0)
    w0 = pl.multiple_of(jnp.maximum(q0 + tq - WIN_CHUNKS * SEL_CHUNK, 0), tq)
    m = jnp.full((rows, LANES), NEG_INF, F32)
    for d in range(WIN_CHUNKS):
        k0 = pl.multiple_of(w0 + d * SEL_CHUNK, tq)
        s = lax.dot_general(lhs_w, kw_ref[0, pl.ds(k0, SEL_CHUNK), :], (((1,), (1,)), ((), ())),
                            preferred_element_type=F32)
        key = k0 + col
        s = jnp.where((key <= t_rows) & (key > t_rows - WINDOW), s, NEG_INF)
        sw_ref[d] = s
        m = jnp.maximum(m, jnp.maximum(s[:, :LANES], s[:, LANES:]))
    m = jnp.broadcast_to(jnp.max(m, axis=1, keepdims=True), (rows, LANES))
    m2 = jnp.concatenate([m, m], axis=1)
    lw = jnp.zeros((rows, LANES), F32)
    accw = jnp.zeros((rows, LANES), F32)
    for d in range(WIN_CHUNKS):
        k0 = pl.multiple_of(w0 + d * SEL_CHUNK, tq)
        p = jnp.exp(sw_ref[d] - m2)
        lw = lw + p[:, :LANES] + p[:, LANES:]
        accw = accw + jnp.dot(p.astype(BF16), vw_ref[0, pl.ds(k0, SEL_CHUNK), :], preferred_element_type=F32)
    o_win = accw / jnp.sum(lw, axis=1, keepdims=True)

    res = []
    for h in range(NSA_HEADS):
        r = slice(h * tq, (h + 1) * tq)
        g = gate_ref[0]
        res.append(g[:, 3 * h:3 * h + 1] * o_cmp[h] + g[:, 3 * h + 1:3 * h + 2] * o_slc[r]
                   + g[:, 3 * h + 2:3 * h + 3] * o_win[r])
    for j in range(NSA_HEADS // 2):
        a, b = res[2 * j], res[2 * j + 1]
        if (2 * j) // HEADS_PER_GROUP == 0:
            b = pltpu.roll(b, HEAD_DIM, 1)
        else:
            a = pltpu.roll(a, HEAD_DIM, 1)
        o_ref[0, :, j * LANES:(j + 1) * LANES] = jnp.where(lane < HEAD_DIM, a, b)


def _nsa(q3, qr3, gate3, kcmp, vcmp, ks3, vs3, kw3, vw3, ov_rep):
    bsz, seq, qw = q3.shape
    assert seq % SEL_CHUNK == 0 and seq >= WIN_CHUNKS * SEL_CHUNK and LANES % (seq // SEL_BLOCK) == 0
    assert seq // CMP_STRIDE == LANES, "compressed tokens are indexed by lane"
    tq = NSA_TQ
    rows = NSA_HEADS * tq
    tile = lambda w: pl.BlockSpec((1, tq, w), lambda bi, qi: (bi, qi, 0))
    per_b = lambda a: pl.BlockSpec((1,) + a.shape[1:], lambda bi, qi: (bi, 0, 0))
    return pl.pallas_call(
        functools.partial(_nsa_body, seq=seq),
        grid=(bsz, seq // tq),
        in_specs=[tile(qw), tile(qw), tile(LANES), per_b(kcmp), per_b(vcmp), per_b(ks3), per_b(vs3),
                  per_b(kw3), per_b(vw3), pl.BlockSpec(ov_rep.shape, lambda bi, qi: (0, 0))],
        out_specs=tile(NSA_HEADS * HEAD_DIM),
        out_shape=jax.ShapeDtypeStruct((bsz, seq, NSA_HEADS * HEAD_DIM), F32),
        scratch_shapes=[
            pltpu.VMEM((seq, 2 * LANES), BF16),
            pltpu.VMEM((seq // SEL_CHUNK, rows, SEL_CHUNK), F32),
            pltpu.VMEM((WIN_CHUNKS, rows, SEL_CHUNK), F32),
            pltpu.VMEM((rows, LANES), F32),
            pltpu.VMEM((rows, LANES), F32),
            pltpu.VMEM((rows, LANES), F32),
        ],
        compiler_params=pltpu.CompilerParams(dimension_semantics=("arbitrary", "arbitrary"), vmem_limit_bytes=VMEM_LIMIT),
        name="nsa",
    )(q3, qr3, gate3, kcmp, vcmp, ks3, vs3, kw3, vw3, ov_rep)


MEM_HEADS = 4
MEM_HEAD_DIM = D_MODEL // MEM_HEADS


def _memkv_body(mem_ref, g_ref, wk_ref, wv_ref, k_ref, v_ref):
    mb = _rms(mem_ref[0], g_ref[...]).astype(BF16)
    k_ref[0] = jnp.dot(mb, wk_ref[...], preferred_element_type=F32).astype(BF16)
    v_ref[0] = jnp.dot(mb, wv_ref[...], preferred_element_type=F32).astype(BF16)


def _memkv(mem, g, wk, wv):
    bsz, mlen, _ = mem.shape
    full = lambda a: pl.BlockSpec(a.shape, lambda bi: (0,) * a.ndim)
    blk = pl.BlockSpec((1, mlen, D_MODEL), lambda bi: (bi, 0, 0))
    return pl.pallas_call(
        _memkv_body,
        grid=(bsz,),
        in_specs=[blk, full(g), full(wk), full(wv)],
        out_specs=[blk, blk],
        out_shape=[jax.ShapeDtypeStruct((bsz, mlen, D_MODEL), BF16)] * 2,
        compiler_params=pltpu.CompilerParams(dimension_semantics=("arbitrary",), vmem_limit_bytes=VMEM_LIMIT),
        name="memkv",
    )(mem, g, wk, wv)


def _mid_body(x_ref, yc_ref, yn_ref, wo_ref, gq_ref, wq_ref, km_ref, vm_ref, wmo_ref, gp_ref, h_ref, hp_ref):
    h1 = (x_ref[0]
          + jnp.dot(yc_ref[0].astype(BF16), wo_ref[0:CONV_WIDTH, :], preferred_element_type=F32)
          + jnp.dot(yn_ref[0].astype(BF16), wo_ref[CONV_WIDTH:, :], preferred_element_type=F32))
    qm = jnp.dot(_rms(h1, gq_ref[...]).astype(BF16), wq_ref[...], preferred_element_type=F32)
    heads = []
    for hh in range(MEM_HEADS):
        c = slice(hh * MEM_HEAD_DIM, (hh + 1) * MEM_HEAD_DIM)
        s = lax.dot_general(qm[:, c].astype(BF16), km_ref[0, :, c], (((1,), (1,)), ((), ())),
                            preferred_element_type=F32) * (MEM_HEAD_DIM ** -0.5)
        e = jnp.exp(s - jnp.max(s, axis=1, keepdims=True))
        p = e / jnp.sum(e, axis=1, keepdims=True)
        heads.append(jnp.dot(p.astype(BF16), vm_ref[0, :, c], preferred_element_type=F32))
    o = jnp.concatenate(heads, axis=1)
    h2 = h1 + jnp.dot(o.astype(BF16), wmo_ref[...], preferred_element_type=F32)
    h_ref[0] = h2
    hp_ref[0] = _rms(h2, gp_ref[...])


def _mid(x3, yc3, yn3, wo, gq, wq, km, vm, wmo, gp, tm=256):
    bsz, seq, _ = x3.shape
    full = lambda a: pl.BlockSpec(a.shape, lambda bi, ti: (0,) * a.ndim)
    tile = lambda w: pl.BlockSpec((1, tm, w), lambda bi, ti: (bi, ti, 0))
    per_b = lambda a: pl.BlockSpec((1,) + a.shape[1:], lambda bi, ti: (bi, 0, 0))
    out = jax.ShapeDtypeStruct((bsz, seq, D_MODEL), F32)
    return pl.pallas_call(
        _mid_body,
        grid=(bsz, seq // tm),
        in_specs=[tile(D_MODEL), tile(CONV_WIDTH), tile(CONV_WIDTH), full(wo), full(gq), full(wq),
                  per_b(km), per_b(vm), full(wmo), full(gp)],
        out_specs=[tile(D_MODEL), tile(D_MODEL)],
        out_shape=[out, out],
        compiler_params=pltpu.CompilerParams(dimension_semantics=("arbitrary", "arbitrary"), vmem_limit_bytes=VMEM_LIMIT),
        name="mid",
    )(x3, yc3, yn3, wo, gq, wq, km, vm, wmo, gp)


PEER_HEADS = 8
PEER_KEYS = 128
PEER_HALF = 128
PEER_TOPK = 16
PEER_SLOTS = PEER_HEADS * PEER_TOPK


def _topk_rows(s, k, payload=None):
    nrow, tm = s.shape
    riota = lax.broadcasted_iota(jnp.int32, (nrow, tm), 0)
    kiota = lax.broadcasted_iota(jnp.int32, (k, tm), 0)
    vals = jnp.zeros((k, tm), F32)
    picks = jnp.zeros((k, tm), jnp.int32)
    for it in range(k):
        m = jnp.max(s, axis=0, keepdims=True)
        idx = jnp.min(jnp.where(s == m, riota, nrow), axis=0, keepdims=True)
        hit = riota == idx
        pick = idx if payload is None else jnp.sum(jnp.where(hit, payload, 0), axis=0, keepdims=True)
        vals = jnp.where(kiota == it, m, vals)
        picks = jnp.where(kiota == it, pick, picks)
        s = jnp.where(hit, -jnp.inf, s)
    return vals, picks


def _peer_topk_body(hp_ref, wqt_ref, keys_ref, ex_ref, gate_ref):
    hb = hp_ref[...].astype(BF16)
    qt = lax.dot_general(wqt_ref[...], hb, (((1,), (1,)), ((), ())), preferred_element_type=F32)
    for h in range(PEER_HEADS):
        top = []
        for p in range(2):
            c = 2 * h + p
            qc = qt[c * PEER_HALF:(c + 1) * PEER_HALF, :].astype(BF16)
            s = jnp.dot(keys_ref[c], qc, preferred_element_type=F32)
            top.append(_topk_rows(s, PEER_TOPK))
        (v0, i0), (v1, i1) = top
        cand = jnp.concatenate([v0[i:i + 1, :] + v1 for i in range(PEER_TOPK)], axis=0)
        expert = jnp.concatenate([i0[i:i + 1, :] * PEER_KEYS + i1 for i in range(PEER_TOPK)], axis=0)
        best, ex = _topk_rows(cand, PEER_TOPK, payload=expert)
        e = jnp.exp(best - jnp.max(best, axis=0, keepdims=True))
        r = slice(h * PEER_TOPK, (h + 1) * PEER_TOPK)
        ex_ref[0, r, :] = ex
        gate_ref[0, r, :] = e / jnp.sum(e, axis=0, keepdims=True)


def _peer_topk(hp2, wqt, keys, tm=128):
    n = hp2.shape[0]
    full = lambda a: pl.BlockSpec(a.shape, lambda i: (0,) * a.ndim)
    out = pl.BlockSpec((1, PEER_SLOTS, tm), lambda i: (i, 0, 0))
    return pl.pallas_call(
        _peer_topk_body,
        grid=(n // tm,),
        in_specs=[pl.BlockSpec((tm, D_MODEL), lambda i: (i, 0)), full(wqt), full(keys)],
        out_specs=[out, out],
        out_shape=[jax.ShapeDtypeStruct((n // tm, PEER_SLOTS, tm), jnp.int32),
                   jax.ShapeDtypeStruct((n // tm, PEER_SLOTS, tm), F32)],
        compiler_params=pltpu.CompilerParams(dimension_semantics=("arbitrary",), vmem_limit_bytes=VMEM_LIMIT),
        name="peer_topk",
    )(hp2, wqt, keys)


PEER_TG = 8


def _peer_body(idx_hbm, uv_hbm, hp_ref, gate_ref, h_ref, gf_ref, o_ref,
               idx_smem, buf0, buf1, sem_i, sem_r, *, nsteps):
    i = pl.program_id(0)
    tg = PEER_TG
    per = tg * PEER_SLOTS
    bufs = (buf0, buf1)
    last_tile = 2 * nsteps - 1

    def idx_copy(tile, s):
        return pltpu.make_async_copy(idx_hbm.at[pl.ds(tile * per, per)], idx_smem.at[s], sem_i.at[s])

    def issue_token(s, n):
        for k in range(PEER_SLOTS):
            e = idx_smem[s, n * PEER_SLOTS + k]
            pltpu.make_async_copy(uv_hbm.at[pl.ds(e, 1), :], bufs[s].at[n, pl.ds(k, 1), :],
                                  sem_r.at[s]).start(priority=k % 2)

    def wait_rows(s):
        pltpu.make_async_copy(bufs[s], bufs[s], sem_r.at[s]).wait()

    def eval_token(s, n, row):
        xr = jnp.broadcast_to(hp_ref[row:row + 1, :], (SUBLANES, D_MODEL)).astype(BF16)
        u = bufs[s][n, :, 0:D_MODEL].astype(BF16)
        a = lax.dot_general(xr, u, (((1,), (1,)), ((), ())), preferred_element_type=F32)
        ga = (jax.nn.gelu(a) * gate_ref[row:row + 1, :]).astype(BF16)
        v = bufs[s][n, :, D_MODEL:2 * D_MODEL].astype(BF16)
        return jnp.dot(ga, v, preferred_element_type=F32)[0:1, :]

    @pl.when(i == 0)
    def _():
        idx_copy(0, 0).start()
        idx_copy(0, 0).wait()
        for n in range(tg):
            issue_token(0, n)
        idx_copy(1, 1).start()

    rows = []
    idx_copy(jnp.minimum(2 * i + 2, last_tile), 0).start()
    idx_copy(2 * i + 1, 1).wait()
    wait_rows(0)
    for n in range(tg):
        issue_token(1, n)
        rows.append(eval_token(0, n, n))
    idx_copy(jnp.minimum(2 * i + 3, last_tile), 1).start()
    idx_copy(jnp.minimum(2 * i + 2, last_tile), 0).wait()
    wait_rows(1)
    for n in range(tg):
        issue_token(0, n)
        rows.append(eval_token(1, n, tg + n))
    o_ref[...] = _rms(h_ref[...] + jnp.concatenate(rows, axis=0), gf_ref[...])

    @pl.when(i == nsteps - 1)
    def _():
        idx_copy(last_tile, 1).wait()
        wait_rows(0)


def _peer(idx_flat, uv, hp2, gates2, h2, gf):
    n = hp2.shape[0]
    tg = PEER_TG
    nsteps = n // (2 * tg)
    tile = lambda w: pl.BlockSpec((2 * tg, w), lambda i: (i, 0))
    anyspace = pl.BlockSpec(memory_space=pl.ANY)
    return pl.pallas_call(
        functools.partial(_peer_body, nsteps=nsteps),
        grid=(nsteps,),
        in_specs=[anyspace, anyspace, tile(D_MODEL), tile(PEER_SLOTS), tile(D_MODEL),
                  pl.BlockSpec(gf.shape, lambda i: (0, 0))],
        out_specs=tile(D_MODEL),
        out_shape=jax.ShapeDtypeStruct((n, D_MODEL), F32),
        scratch_shapes=[
            pltpu.SMEM((2, tg * PEER_SLOTS), jnp.int32),
            pltpu.VMEM((tg, PEER_SLOTS, 2 * D_MODEL), F32),
            pltpu.VMEM((tg, PEER_SLOTS, 2 * D_MODEL), F32),
            pltpu.SemaphoreType.DMA((2,)),
            pltpu.SemaphoreType.DMA((2,)),
        ],
        compiler_params=pltpu.CompilerParams(dimension_semantics=("arbitrary",), vmem_limit_bytes=VMEM_LIMIT),
        name="peer",
    )(idx_flat, uv, hp2, gates2, h2, gf)


def kernel(x, mem, positions, mix_norm_g, w_in, conv_dw_w, conv_dw_b, conv_ln_g, conv_ln_b, cmp_pos, cmp_w1, cmp_b1,
           cmp_w2, cmp_b2, w_out, mem_q_norm_g, mem_kv_norm_g, w_mem_q, w_mem_k, w_mem_v, w_mem_o, peer_norm_g,
           peer_w_q, peer_sub_keys, peer_u, peer_v, final_norm_g):
    bsz, seq, d = x.shape
    n = bsz * seq
    depth = w_in.shape[0]
    assert depth == 1, "the final rmsnorm is fused into the last layer's PEER kernel"
    row = lambda a: a.reshape(1, -1)
    l = 0
    wab, wq_pad, wkv, wg_pad, inv_lane, posb = _prep_inproj(w_in[l], positions)
    u, q, qr, kc, vc, ks, vs, kw, vw, gate = _inproj(
        x.reshape(n, d), posb, row(mix_norm_g[l]), inv_lane, wab, wq_pad, wkv, wg_pad)
    y_conv = _conv(u.reshape(bsz, seq, CONV_WIDTH), conv_dw_w[l], row(conv_dw_b[l]), row(conv_ln_g[l]),
                   row(conv_ln_b[l]))
    stride_rows = lambda a: a.reshape(bsz, seq // CMP_STRIDE, CMP_STRIDE * KV_WIDTH)
    kcmp, vcmp = _compress(stride_rows(kc), stride_rows(vc),
                           *_prep_compress(cmp_pos[l], cmp_w1[l], cmp_b1[l], cmp_w2[l], cmp_b2[l]))
    b3 = lambda a: a.reshape(bsz, seq, a.shape[-1])
    y_nsa = _nsa(b3(q), b3(qr), b3(gate), kcmp, vcmp, b3(ks), b3(vs), b3(kw), b3(vw), _overlap_table(seq))
    km, vm = _memkv(mem, row(mem_kv_norm_g[l]), w_mem_k[l].astype(BF16), w_mem_v[l].astype(BF16))
    h2, hp = _mid(x, y_conv, y_nsa, w_out[l].astype(BF16), row(mem_q_norm_g[l]), w_mem_q[l].astype(BF16), km, vm,
                  w_mem_o[l].astype(BF16), row(peer_norm_g[l]))
    keys = peer_sub_keys[l].reshape(2 * PEER_HEADS, PEER_KEYS, PEER_HALF).astype(BF16)
    ex_t, gate_t = _peer_topk(hp.reshape(n, d), peer_w_q[l].T.astype(BF16), keys)
    token_major = lambda a: jnp.swapaxes(a, 1, 2).reshape(n, PEER_SLOTS)
    uv = jnp.concatenate([peer_u[l], peer_v[l]], axis=1)
    out = _peer(token_major(ex_t).reshape(n * PEER_SLOTS), uv, hp.reshape(n, d), token_major(gate_t),
                h2.reshape(n, d), row(final_norm_g))
    return out.reshape(bsz, seq, d)


def _overlap_table(seq):
    n_blk = seq // SEL_BLOCK
    n_cmp = (seq - CMP_BLOCK) // CMP_STRIDE + 1
    c = jnp.arange(LANES)[:, None]
    s = (jnp.arange(LANES) % n_blk)[None, :]
    ov = ((c * CMP_STRIDE <= s * SEL_BLOCK + SEL_BLOCK - 1) & (c * CMP_STRIDE + CMP_BLOCK - 1 >= s * SEL_BLOCK)
          & (c < n_cmp))
    return ov.astype(BF16)
```
